```python
import math
import jax, jax.numpy as jnp
from jax import lax
import numpy as np

D_MODEL = 1024
BATCH = 2
SEQ = 8192
DEPTH = 2

HEAD_DIM = 64
A_Q_HEADS = 8
A_KV_HEADS = 2
B_HEADS = 8
B_BRANCHES = ((128, 1), (512, 4), (2048, 16))
C_HEADS = 8
C_V_DIM = 2 * HEAD_DIM
D_FF = ((-(-8 * D_MODEL // 3) + 255) // 256) * 256
GRID_W = 64
ROPE_THETA = 10000.0
Q_BLOCK = 128
EPS = 1e-6

N_EVEN = (DEPTH + 1) // 2
N_ODD = DEPTH // 2

A_Q_W = A_Q_HEADS * HEAD_DIM
A_KV_W = A_KV_HEADS * HEAD_DIM
B_W = B_HEADS * HEAD_DIM
EVEN_SPLITS = tuple(int(v) for v in np.cumsum([A_Q_W, A_KV_W, A_KV_W, B_W, B_W]))
EVEN_IN_W = A_Q_W + 2 * A_KV_W + 3 * B_W
EVEN_OUT_W = A_Q_W + B_W
C_QK_W = C_HEADS * 2 * HEAD_DIM
C_V_W = C_HEADS * C_V_DIM
ODD_SPLITS = (C_QK_W, 2 * C_QK_W)
ODD_IN_W = 2 * C_QK_W + C_V_W
ODD_OUT_W = C_V_W

kernel_name = "hybrid_gqa_dilated_diffattn_encoder"


def rms_norm(x, g):
    xf = x.astype(jnp.float32)
    y = xf * lax.rsqrt(jnp.mean(xf * xf, axis=-1, keepdims=True) + EPS)
    return (y * g.astype(jnp.float32)).astype(x.dtype)


def rope_angles(pos, dim):
    inv = ROPE_THETA ** (-jnp.arange(0, dim, 2, dtype=jnp.float32) / dim)
    return pos.astype(jnp.float32)[:, None] * inv[None, :]


def apply_rope(x, ang):
    cos = jnp.cos(ang)[None, :, None, :]
    sin = jnp.sin(ang)[None, :, None, :]
    x1, x2 = jnp.split(x.astype(jnp.float32), 2, axis=-1)
    out = jnp.concatenate([x1 * cos - x2 * sin, x1 * sin + x2 * cos], axis=-1)
    return out.astype(x.dtype)


def gqa_attention(q, k, v):
    bn, s, hq, d = q.shape
    hkv = k.shape[2]
    g = hq // hkv
    nb = s // Q_BLOCK
    scale = d ** -0.5
    qb = q.reshape(bn, nb, Q_BLOCK, hkv, g, d).transpose(1, 0, 2, 3, 4, 5)

    def block(qblk):
        sc = jnp.einsum('bqkgd,bskd->bkgqs', qblk, k).astype(jnp.float32) * scale
        p = jax.nn.softmax(sc, axis=-1).astype(v.dtype)
        return jnp.einsum('bkgqs,bskd->bqkgd', p, v)

    o = lax.map(block, qb)
    return o.transpose(1, 0, 2, 3, 4, 5).reshape(bn, s, hq * d)


def dilated_attention(q, k, v):
    bn, s, h, d = q.shape
    nb = s // Q_BLOCK
    scale = d ** -0.5
    neg = jnp.finfo(jnp.float32).min

    def block(b_idx):
        start = b_idx * Q_BLOCK
        t = start + jnp.arange(Q_BLOCK, dtype=jnp.int32)
        qblk = lax.dynamic_slice_in_dim(q, start, Q_BLOCK, axis=1)
        outs, lses = [], []
        for window, dil in B_BRANCHES:
            half = window // (2 * dil)
            offs = dil * jnp.arange(-half, half + 1, dtype=jnp.int32)
            idx = t[:, None] + offs[None, :]
            valid = (idx >= 0) & (idx < s)
            idx_c = jnp.clip(idx, 0, s - 1)
            kg = k[:, idx_c]
            vg = v[:, idx_c]
            sc = jnp.einsum('bqhd,bqkhd->bhqk', qblk, kg).astype(jnp.float32) * scale
            sc = jnp.where(valid[None, None], sc, neg)
            lse = jax.nn.logsumexp(sc, axis=-1)
            p = jnp.exp(sc - lse[..., None]).astype(v.dtype)
            outs.append(jnp.einsum('bhqk,bqkhd->bqhd', p, vg))
            lses.append(lse)
        w = jax.nn.softmax(jnp.stack(lses, axis=0), axis=0)
        w = w.transpose(0, 1, 3, 2)[..., None].astype(v.dtype)
        return jnp.sum(w * jnp.stack(outs, axis=0), axis=0)

    o = lax.map(block, jnp.arange(nb, dtype=jnp.int32))
    return o.transpose(1, 0, 2, 3, 4).reshape(bn, s, h * d)


def diff_attention(q1, q2, k1, k2, v, lam):
    bn, s, h, d = q1.shape
    nb = s // Q_BLOCK
    scale = d ** -0.5
    q1b = q1.reshape(bn, nb, Q_BLOCK, h, d).transpose(1, 0, 2, 3, 4)
    q2b = q2.reshape(bn, nb, Q_BLOCK, h, d).transpose(1, 0, 2, 3, 4)

    def block(qs):
        qa, qb = qs
        s1 = jnp.einsum('bqhd,bshd->bhqs', qa, k1).astype(jnp.float32) * scale
        s2 = jnp.einsum('bqhd,bshd->bhqs', qb, k2).astype(jnp.float32) * scale
        p = jax.nn.softmax(s1, axis=-1) - lam * jax.nn.softmax(s2, axis=-1)
        return jnp.einsum('bhqs,bshe->bqhe', p.astype(v.dtype), v)

    o = lax.map(block, (q1b, q2b))
    return o.transpose(1, 0, 2, 3, 4).reshape(bn, s, h, v.shape[-1])


def swiglu(h, w_gate, w_up, w_down):
    return (jax.nn.silu(h @ w_gate) * (h @ w_up)) @ w_down


def setup_inputs(seed: int = 0) -> dict:
    key = jax.random.key(seed)
    ks = jax.random.split(key, 20)

    def dense(k, shape):
        return jax.random.normal(k, shape, jnp.float32) * shape[-2] ** -0.5

    def gain(k, shape):
        return 1.0 + 0.02 * jax.random.normal(k, shape, jnp.float32)

    return {
        "x": jax.random.normal(ks[0], (BATCH, SEQ, D_MODEL), jnp.float32),
        "attn_norm": gain(ks[1], (DEPTH, D_MODEL)),
        "ffn_norm": gain(ks[2], (DEPTH, D_MODEL)),
        "final_norm": gain(ks[3], (D_MODEL,)),
        "w_in_even": dense(ks[4], (N_EVEN, D_MODEL, EVEN_IN_W)),
        "a_q_norm": gain(ks[5], (N_EVEN, HEAD_DIM)),
        "a_k_norm": gain(ks[6], (N_EVEN, HEAD_DIM)),
        "w_out_even": dense(ks[7], (N_EVEN, EVEN_OUT_W, D_MODEL)),
        "w_in_odd": dense(ks[8], (N_ODD, D_MODEL, ODD_IN_W)),
        "lambda_q1": 0.1 * jax.random.normal(ks[9], (N_ODD, HEAD_DIM), jnp.float32),
        "lambda_k1": 0.1 * jax.random.normal(ks[10], (N_ODD, HEAD_DIM), jnp.float32),
        "lambda_q2": 0.1 * jax.random.normal(ks[11], (N_ODD, HEAD_DIM), jnp.float32),
        "lambda_k2": 0.1 * jax.random.normal(ks[12], (N_ODD, HEAD_DIM), jnp.float32),
        "c_sub_norm": gain(ks[13], (N_ODD, C_V_DIM)),
        "w_out_odd": dense(ks[14], (N_ODD, ODD_OUT_W, D_MODEL)),
        "w_gate": dense(ks[15], (DEPTH, D_MODEL, D_FF)),
        "w_up": dense(ks[16], (DEPTH, D_MODEL, D_FF)),
        "w_down": dense(ks[17], (DEPTH, D_FF, D_MODEL)),
    }


def reference(x, attn_norm, ffn_norm, final_norm, w_in_even, a_q_norm, a_k_norm, w_out_even,
              w_in_odd, lambda_q1, lambda_k1, lambda_q2, lambda_k2, c_sub_norm, w_out_odd,
              w_gate, w_up, w_down):
    bn, s, _ = x.shape
    rows = s // GRID_W
    row_ids = jnp.repeat(jnp.arange(rows, dtype=jnp.int32), GRID_W)
    col_ids = jnp.tile(jnp.arange(GRID_W, dtype=jnp.int32), rows)
    pos = jnp.arange(rows * GRID_W, dtype=jnp.int32)
    ang_1d = rope_angles(pos, HEAD_DIM)
    ang_2d = jnp.concatenate([rope_angles(row_ids, HEAD_DIM // 2),
                              rope_angles(col_ids, HEAD_DIM // 2)], axis=-1)

    for i in range(DEPTH):
        j = i // 2
        h = rms_norm(x, attn_norm[i])
        if i % 2 == 0:
            proj = h @ w_in_even[j]
            aq, ak, av, bq, bk, bv = jnp.split(proj, EVEN_SPLITS, axis=-1)
            aq = aq.reshape(bn, s, A_Q_HEADS, HEAD_DIM)
            ak = ak.reshape(bn, s, A_KV_HEADS, HEAD_DIM)
            av = av.reshape(bn, s, A_KV_HEADS, HEAD_DIM)
            aq = apply_rope(rms_norm(aq, a_q_norm[j]), ang_2d)
            ak = apply_rope(rms_norm(ak, a_k_norm[j]), ang_2d)
            bq = apply_rope(bq.reshape(bn, s, B_HEADS, HEAD_DIM), ang_1d)
            bk = apply_rope(bk.reshape(bn, s, B_HEADS, HEAD_DIM), ang_1d)
            bv = bv.reshape(bn, s, B_HEADS, HEAD_DIM)
            mix = jnp.concatenate([gqa_attention(aq, ak, av),
                                   dilated_attention(bq, bk, bv)], axis=-1)
            x = x + mix @ w_out_even[j]
        else:
            proj = h @ w_in_odd[j]
            q, k, v = jnp.split(proj, ODD_SPLITS, axis=-1)
            q = q.reshape(bn, s, C_HEADS, 2, HEAD_DIM)
            k = k.reshape(bn, s, C_HEADS, 2, HEAD_DIM)
            v = v.reshape(bn, s, C_HEADS, C_V_DIM)
            q1 = apply_rope(q[..., 0, :], ang_1d)
            q2 = apply_rope(q[..., 1, :], ang_1d)
            k1 = apply_rope(k[..., 0, :], ang_1d)
            k2 = apply_rope(k[..., 1, :], ang_1d)
            lam_init = 0.8 - 0.6 * math.exp(-0.3 * i)
            lam = (jnp.exp(jnp.sum(lambda_q1[j].astype(jnp.float32) * lambda_k1[j].astype(jnp.float32)))
                   - jnp.exp(jnp.sum(lambda_q2[j].astype(jnp.float32) * lambda_k2[j].astype(jnp.float32)))
                   + lam_init)
            o = diff_attention(q1, q2, k1, k2, v, lam)
            o = rms_norm(o, c_sub_norm[j]) * (1.0 - lam_init)
            x = x + o.reshape(bn, s, ODD_OUT_W) @ w_out_odd[j]
        h = rms_norm(x, ffn_norm[i])
        x = x + swiglu(h, w_gate[i], w_up[i], w_down[i])
    return rms_norm(x, final_norm)
```

```python
import functools
import math

import jax
import jax.numpy as jnp
from jax import lax
from jax.experimental import pallas as pl
from jax.experimental.pallas import tpu as pltpu

F32 = jnp.float32
BF16 = jnp.bfloat16

HEAD_DIM = 64
LANES = 128
EPS = 1e-6
ROPE_THETA = 10000.0
GRID_W = 64
A_Q_HEADS = 8
B_HEADS = 8
C_HEADS = 8
DIL_BRANCHES = ((128, 1), (512, 4), (2048, 16))
DIL_HALF = 64
DIL_TILE = 2048
NEG_BIG = -1e30
VMEM_LIMIT = 56 * 1024 * 1024


def _cparams(sem):
    return pltpu.CompilerParams(dimension_semantics=sem, vmem_limit_bytes=VMEM_LIMIT)


def _rms_rows(x, g):
    return x * lax.rsqrt(jnp.mean(x * x, axis=-1, keepdims=True) + EPS) * g


def _rope(y, cos, sin_signed):
    lane = lax.broadcasted_iota(jnp.int32, y.shape, 1)
    first_half = (lane % HEAD_DIM) < (HEAD_DIM // 2)
    partner = jnp.where(first_half, pltpu.roll(y, LANES - 32, 1), pltpu.roll(y, 32, 1))
    return y * cos + partner * sin_signed


def _lane_lt64(shape):
    return lax.broadcasted_iota(jnp.int32, shape, 1) < HEAD_DIM


def _head_mean_sq(y, bd):
    y2 = y * y
    hi = y2.astype(BF16)
    lo = (y2 - hi.astype(F32)).astype(BF16)
    tot = jnp.dot(hi, bd, preferred_element_type=F32) + jnp.dot(lo, bd, preferred_element_type=F32)
    return tot * (1.0 / HEAD_DIM)


def _proj_even_kernel(x_ref, g_ref, w_ref, gq_ref, gk_ref, bd_ref, c2_ref, s2_ref, c1_ref, s1_ref,
                      aq_ref, akT_ref, av_ref, bq_ref, bk_ref, bv_ref):
    h = _rms_rows(x_ref[...], g_ref[...]).astype(BF16)
    bd = bd_ref[...]
    c2, s2, c1, s1 = c2_ref[...], s2_ref[...], c1_ref[...], s1_ref[...]
    scale = HEAD_DIM ** -0.5
    lt64 = _lane_lt64((x_ref.shape[0], LANES))

    def mm(c0, width):
        return jnp.dot(h, w_ref[:, c0:c0 + width], preferred_element_type=F32)

    yq = mm(0, 512)
    for c in range(4):
        y = yq[:, c * LANES:(c + 1) * LANES]
        y = y * lax.rsqrt(_head_mean_sq(y, bd) + EPS) * gq_ref[...]
        y = _rope(y, c2, s2) * scale
        ysw = pltpu.roll(y, HEAD_DIM, 1)
        if c < 2:
            even, odd = jnp.where(lt64, y, 0.0), jnp.where(lt64, ysw, 0.0)
        else:
            even, odd = jnp.where(lt64, 0.0, ysw), jnp.where(lt64, 0.0, y)
        aq_ref[2 * c] = even.astype(BF16)
        aq_ref[2 * c + 1] = odd.astype(BF16)

    ykv = mm(512, 256)
    yk = ykv[:, :LANES]
    yk = yk * lax.rsqrt(_head_mean_sq(yk, bd) + EPS) * gk_ref[...]
    yk = _rope(yk, c2, s2)
    akT_ref[0] = yk.T.astype(BF16)
    av_ref[...] = ykv[:, LANES:].astype(BF16)

    ybq = mm(768, 512)
    ybk = mm(1280, 512)
    for c in range(4):
        sl = slice(c * LANES, (c + 1) * LANES)
        bq_ref[:, sl] = _rope(ybq[:, sl], c1, s1) * scale
        bk_ref[:, sl] = _rope(ybk[:, sl], c1, s1)
    bv_ref[...] = mm(1792, 512)


def _proj_even(x2, g, w, gq, gk, bd, tabs, bsz, seq, tm):
    n, d = x2.shape
    ns = seq // tm
    c2, s2, c1, s1 = tabs
    row = lambda i: (i, 0)
    tab = pl.BlockSpec((tm, LANES), lambda i: (i % ns, 0))
    const = lambda shape: pl.BlockSpec(shape, lambda i: tuple(0 for _ in shape))
    return pl.pallas_call(
        _proj_even_kernel,
        grid=(n // tm,),
        in_specs=[pl.BlockSpec((tm, d), row), const((1, d)), const(w.shape), const((1, LANES)),
                  const((1, LANES)), const((LANES, LANES)), tab, tab, tab, tab],
        out_specs=[pl.BlockSpec((A_Q_HEADS, tm, LANES), lambda i: (0, i, 0)),
                   pl.BlockSpec((1, LANES, tm), lambda i: (i // ns, 0, i % ns)),
                   pl.BlockSpec((tm, LANES), row),
                   pl.BlockSpec((tm, 512), row), pl.BlockSpec((tm, 512), row), pl.BlockSpec((tm, 512), row)],
        out_shape=[jax.ShapeDtypeStruct((A_Q_HEADS, n, LANES), BF16),
                   jax.ShapeDtypeStruct((bsz, LANES, seq), BF16),
                   jax.ShapeDtypeStruct((n, LANES), BF16),
                   jax.ShapeDtypeStruct((n, 512), F32), jax.ShapeDtypeStruct((n, 512), F32),
                   jax.ShapeDtypeStruct((n, 512), F32)],
        compiler_params=_cparams(("parallel",)),
        name="proj_even",
    )(x2, g, w, gq, gk, bd, c2, s2, c1, s1)


def _proj_odd_kernel(x_ref, g_ref, w_ref, c1_ref, s1_ref, q_ref, kT_ref, v_ref):
    h = _rms_rows(x_ref[...], g_ref[...]).astype(BF16)
    c1, s1 = c1_ref[...], s1_ref[...]
    scale = HEAD_DIM ** -0.5
    lt64 = _lane_lt64((x_ref.shape[0], LANES))
    width = C_HEADS * LANES

    def mm(c0, w):
        return jnp.dot(h, w_ref[:, c0:c0 + w], preferred_element_type=F32)

    for half in range(2):
        yq = mm(half * 512, 512)
        yk = mm(width + half * 512, 512)
        for c in range(4):
            hd = half * 4 + c
            sl = slice(c * LANES, (c + 1) * LANES)
            y = _rope(yq[:, sl], c1, s1) * scale
            q_ref[0, :, hd * LANES:(hd + 1) * LANES] = jnp.where(lt64, y, 0.0).astype(BF16)
            q_ref[1, :, hd * LANES:(hd + 1) * LANES] = jnp.where(lt64, 0.0, y).astype(BF16)
            kT_ref[0, hd * LANES:(hd + 1) * LANES, :] = _rope(yk[:, sl], c1, s1).T.astype(BF16)
        v_ref[:, half * 512:(half + 1) * 512] = mm(2 * width + half * 512, 512).astype(BF16)


def _proj_odd(x2, g, w, tabs, bsz, seq, tm):
    n, d = x2.shape
    ns = seq // tm
    c1, s1 = tabs
    width = C_HEADS * LANES
    row = lambda i: (i, 0)
    tab = pl.BlockSpec((tm, LANES), lambda i: (i % ns, 0))
    const = lambda shape: pl.BlockSpec(shape, lambda i: tuple(0 for _ in shape))
    return pl.pallas_call(
        _proj_odd_kernel,
        grid=(n // tm,),
        in_specs=[pl.BlockSpec((tm, d), row), const((1, d)), const(w.shape), tab, tab],
        out_specs=[pl.BlockSpec((2, tm, width), lambda i: (0, i, 0)),
                   pl.BlockSpec((1, width, tm), lambda i: (i // ns, 0, i % ns)),
                   pl.BlockSpec((tm, width), row)],
        out_shape=[jax.ShapeDtypeStruct((2, n, width), BF16),
                   jax.ShapeDtypeStruct((bsz, width, seq), BF16),
                   jax.ShapeDtypeStruct((n, width), BF16)],
        compiler_params=_cparams(("parallel",)),
        name="proj_odd",
    )(x2, g, w, c1, s1)


def _flash_rows(q, kT_ref, v_ref, tk):
    rows = q.shape[0]
    seq = v_ref.shape[0]

    def body(c, carry):
        m, l, acc = carry
        off = pl.multiple_of(c * tk, tk)
        s = jnp.dot(q, kT_ref[0, :, pl.ds(off, tk)], preferred_element_type=F32)
        m_new = jnp.maximum(m, jnp.max(s, axis=-1, keepdims=True))
        alpha = jnp.exp(m - m_new)
        p = jnp.exp(s - m_new)
        l = alpha * l + jnp.sum(p, axis=-1, keepdims=True)
        acc = alpha * acc + jnp.dot(p.astype(BF16), v_ref[pl.ds(off, tk), :], preferred_element_type=F32)
        return m_new, l, acc

    init = (jnp.full((rows, 1), NEG_BIG, F32), jnp.zeros((rows, 1), F32), jnp.zeros((rows, LANES), F32))
    _, l, acc = lax.fori_loop(0, seq // tk, body, init)
    return acc / l


def _gqa_kernel(q_ref, kT_ref, v_ref, o_ref, *, tk):
    nh, tq, _ = q_ref.shape
    o = _flash_rows(q_ref[...].reshape(nh * tq, LANES), kT_ref, v_ref, tk)
    lt64 = _lane_lt64((tq, LANES))
    for c in range(nh // 2):
        even, odd = o[(2 * c) * tq:(2 * c + 1) * tq], o[(2 * c + 1) * tq:(2 * c + 2) * tq]
        if c < nh // 4:
            chunk = jnp.where(lt64, even, pltpu.roll(odd, HEAD_DIM, 1))
        else:
            chunk = jnp.where(lt64, pltpu.roll(even, HEAD_DIM, 1), odd)
        o_ref[:, c * LANES:(c + 1) * LANES] = chunk.astype(o_ref.dtype)


def _gqa_attention(aq, akT, av, bsz, seq, tq, tk):
    nh, n, _ = aq.shape
    nq = seq // tq
    return pl.pallas_call(
        functools.partial(_gqa_kernel, tk=tk),
        grid=(bsz, nq),
        in_specs=[pl.BlockSpec((nh, tq, LANES), lambda b, i: (0, b * nq + i, 0)),
                  pl.BlockSpec((1, LANES, seq), lambda b, i: (b, 0, 0)),
                  pl.BlockSpec((seq, LANES), lambda b, i: (b, 0))],
        out_specs=pl.BlockSpec((tq, nh * HEAD_DIM), lambda b, i: (b * nq + i, 0)),
        out_shape=jax.ShapeDtypeStruct((n, nh * HEAD_DIM), BF16),
        compiler_params=_cparams(("parallel", "arbitrary")),
        name="gqa_attn",
    )(aq, akT, av)


def _diff_kernel(q_ref, kT_ref, v_ref, lam_ref, gsub_ref, o_ref, *, tk, lam_init):
    _, tq, _ = q_ref.shape
    o = _flash_rows(q_ref[...].reshape(2 * tq, LANES), kT_ref, v_ref, tk)
    lv = lam_ref[...]
    lam = (jnp.exp(jnp.sum(lv[0:1] * lv[1:2], axis=-1, keepdims=True))
           - jnp.exp(jnp.sum(lv[2:3] * lv[3:4], axis=-1, keepdims=True)) + lam_init)
    d = o[:tq] - lam * o[tq:]
    d = _rms_rows(d, gsub_ref[...]) * (1.0 - lam_init)
    o_ref[...] = d.astype(o_ref.dtype)


def _diff_attention(cq, ckT, cv, lam_vecs, gsub, bsz, seq, tq, tk, lam_init):
    _, n, width = cq.shape
    nq = seq // tq
    return pl.pallas_call(
        functools.partial(_diff_kernel, tk=tk, lam_init=lam_init),
        grid=(bsz, C_HEADS, nq),
        in_specs=[pl.BlockSpec((2, tq, LANES), lambda b, h, i: (0, b * nq + i, h)),
                  pl.BlockSpec((1, LANES, seq), lambda b, h, i: (b, h, 0)),
                  pl.BlockSpec((seq, LANES), lambda b, h, i: (b, h)),
                  pl.BlockSpec((4, HEAD_DIM), lambda b, h, i: (0, 0)),
                  pl.BlockSpec((1, LANES), lambda b, h, i: (0, 0))],
        out_specs=pl.BlockSpec((tq, LANES), lambda b, h, i: (b * nq + i, h)),
        out_shape=jax.ShapeDtypeStruct((n, width), BF16),
        compiler_params=_cparams(("parallel", "parallel", "arbitrary")),
        name="diff_attn",
    )(cq, ckT, cv, lam_vecs, gsub)


def _dilated_kernel(q_ref, k_ref, v_ref, o_ref, acc_s, m_s, l_s):
    tile = q_ref.shape[1]
    seq = k_ref.shape[1]
    t = pl.program_id(2)
    qblk, win = 128, 256
    lt64 = _lane_lt64((qblk, LANES))
    row = lax.broadcasted_iota(jnp.int32, (qblk, win), 0)
    col = lax.broadcasted_iota(jnp.int32, (qblk, win), 1)
    nt = (((1,), (1,)), ((), ()))

    for bi, (_, d) in enumerate(DIL_BRANCHES):
        cls_len = seq // d
        per_cls = tile // d // qblk

        def blk(j, carry, bi=bi, d=d, cls_len=cls_len, per_cls=per_cls):
            r = j // per_cls
            ub = j % per_cls
            u0 = t * (tile // d) + ub * qblk
            ks = jnp.clip(u0 - DIL_HALF, 0, cls_len - win)
            q = q_ref[0, pl.ds(r + d * (ub * qblk), qblk, stride=d), :]
            kk = k_ref[0, pl.ds(r + d * ks, win, stride=d), :].astype(BF16)
            vv = v_ref[0, pl.ds(r + d * ks, win, stride=d), :].astype(BF16)
            valid = jnp.abs(row + (u0 - ks) - col) <= DIL_HALF
            outs, ms, ls = [], [], []
            for hd in range(2):
                keep = lt64 if hd == 0 else jnp.logical_not(lt64)
                qh = jnp.where(keep, q, 0.0).astype(BF16)
                s = lax.dot_general(qh, kk, nt, preferred_element_type=F32)
                s = jnp.where(valid, s, NEG_BIG)
                m = jnp.max(s, axis=-1, keepdims=True)
                p = jnp.exp(s - m)
                ls.append(jnp.sum(p, axis=-1, keepdims=True))
                ms.append(m)
                outs.append(jnp.dot(p.astype(BF16), vv, preferred_element_type=F32))
            dst = pl.ds(r + d * (ub * qblk), qblk, stride=d)
            acc_s[bi, dst, :] = jnp.where(lt64, outs[0], outs[1])
            m_s[bi, dst, :] = jnp.where(lt64, ms[0], ms[1])
            l_s[bi, dst, :] = jnp.where(lt64, ls[0], ls[1])
            return carry

        lax.fori_loop(0, d * per_cls, blk, 0)

    m_all = jnp.maximum(jnp.maximum(m_s[0], m_s[1]), m_s[2])
    num = jnp.zeros((tile, LANES), F32)
    den = jnp.zeros((tile, LANES), F32)
    for bi in range(len(DIL_BRANCHES)):
        e = jnp.exp(m_s[bi] - m_all)
        num = num + e * acc_s[bi]
        den = den + e * l_s[bi]
    o_ref[0] = (num / den).astype(o_ref.dtype)


def _dilated_attention(bq, bk, bv, tile):
    bsz, seq, width = bq.shape
    nb = len(DIL_BRANCHES)
    return pl.pallas_call(
        _dilated_kernel,
        grid=(bsz, width // LANES, seq // tile),
        in_specs=[pl.BlockSpec((1, tile, LANES), lambda b, h, t: (b, t, h)),
                  pl.BlockSpec((1, seq, LANES), lambda b, h, t: (b, 0, h)),
                  pl.BlockSpec((1, seq, LANES), lambda b, h, t: (b, 0, h))],
        out_specs=pl.BlockSpec((1, tile, LANES), lambda b, h, t: (b, t, h)),
        out_shape=jax.ShapeDtypeStruct((bsz, seq, width), BF16),
        scratch_shapes=[pltpu.VMEM((nb, tile, LANES), F32)] * 3,
        compiler_params=_cparams(("parallel", "parallel", "arbitrary")),
        name="dilated_attn",
    )(bq, bk, bv)


def _ffn_kernel(*refs, n_mix, final):
    x_ref = refs[0]
    mix_refs = refs[1:1 + n_mix]
    wo_ref, g_ref, wg_ref, wu_ref, wd_ref, fg_ref, o_ref = refs[1 + n_mix:]
    mix = mix_refs[0][...] if n_mix == 1 else jnp.concatenate([m[...] for m in mix_refs], axis=1)
    x1 = x_ref[...] + jnp.dot(mix, wo_ref[...], preferred_element_type=F32)
    h = _rms_rows(x1, g_ref[...]).astype(BF16)
    d_ff = wg_ref.shape[1]
    y = x1
    c0 = 0
    while c0 < d_ff:
        cw = min(1024, d_ff - c0)
        gte = jnp.dot(h, wg_ref[:, c0:c0 + cw], preferred_element_type=F32)
        up = jnp.dot(h, wu_ref[:, c0:c0 + cw], preferred_element_type=F32)
        act = (gte / (1.0 + jnp.exp(-gte)) * up).astype(BF16)
        y = y + jnp.dot(act, wd_ref[c0:c0 + cw, :], preferred_element_type=F32)
        c0 += cw
    if final:
        y = _rms_rows(y, fg_ref[...])
    o_ref[...] = y


def _ffn_block(x2, mixes, wo, g, wg, wu, wd, fg, final, tm):
    n, d = x2.shape
    row = lambda i: (i, 0)
    const = lambda shape: pl.BlockSpec(shape, lambda i: (0, 0), pipeline_mode=pl.Buffered(1))
    in_specs = [pl.BlockSpec((tm, d), row)]
    in_specs += [pl.BlockSpec((tm, m.shape[1]), row) for m in mixes]
    in_specs += [const(wo.shape), const((1, d)), const(wg.shape), const(wu.shape), const(wd.shape), const((1, d))]
    return pl.pallas_call(
        functools.partial(_ffn_kernel, n_mix=len(mixes), final=final),
        grid=(n // tm,),
        in_specs=in_specs,
        out_specs=pl.BlockSpec((tm, d), row),
        out_shape=jax.ShapeDtypeStruct((n, d), F32),
        compiler_params=_cparams(("parallel",)),
        name="outproj_ffn",
    )(x2, *mixes, wo, g, wg, wu, wd, fg)


def _rope_tables(ang):
    cos = jnp.cos(ang)
    sin = jnp.sin(ang)
    cos64 = jnp.concatenate([cos, cos], axis=-1)
    sin64 = jnp.concatenate([-sin, sin], axis=-1)
    return jnp.tile(cos64, (1, 2)), jnp.tile(sin64, (1, 2))


def _angles(pos, dim):
    inv = ROPE_THETA ** (-jnp.arange(0, dim, 2, dtype=F32) / dim)
    return pos.astype(F32)[:, None] * inv[None, :]


def kernel(x, attn_norm, ffn_norm, final_norm, w_in_even, a_q_norm, a_k_norm, w_out_even, w_in_odd,
           lambda_q1, lambda_k1, lambda_q2, lambda_k2, c_sub_norm, w_out_odd, w_gate, w_up, w_down):
    bsz, seq, d = x.shape
    n = bsz * seq
    depth = attn_norm.shape[0]
    tm = 512
    assert seq % DIL_TILE == 0 and seq // DIL_BRANCHES[-1][1] >= 256 and seq % GRID_W == 0

    pos = jnp.arange(seq, dtype=jnp.int32)
    c1, s1 = _rope_tables(_angles(pos, HEAD_DIM))
    ang2 = jnp.concatenate([_angles(pos // GRID_W, HEAD_DIM // 2), _angles(pos % GRID_W, HEAD_DIM // 2)], axis=-1)
    c2, s2 = _rope_tables(ang2)
    bd = (jnp.arange(LANES)[:, None] // HEAD_DIM == jnp.arange(LANES)[None, :] // HEAD_DIM).astype(BF16)

    x2 = x.reshape(n, d)
    for i in range(depth):
        j = i // 2
        g_attn = attn_norm[i].reshape(1, d)
        if i % 2 == 0:
            aq, akT, av, bq, bk, bv = _proj_even(
                x2, g_attn, w_in_even[j].astype(BF16), jnp.tile(a_q_norm[j], 2).reshape(1, LANES),
                jnp.tile(a_k_norm[j], 2).reshape(1, LANES), bd, (c2, s2, c1, s1), bsz, seq, tm)
            mix_a = _gqa_attention(aq, akT, av, bsz, seq, tq=128, tk=1024)
            mix_b = _dilated_attention(bq.reshape(bsz, seq, -1), bk.reshape(bsz, seq, -1),
                                       bv.reshape(bsz, seq, -1), DIL_TILE).reshape(n, -1)
            mixes, wo = (mix_a, mix_b), w_out_even[j]
        else:
            cq, ckT, cv = _proj_odd(x2, g_attn, w_in_odd[j].astype(BF16), (c1, s1), bsz, seq, tm)
            lam_init = 0.8 - 0.6 * math.exp(-0.3 * i)
            lam_vecs = jnp.stack([lambda_q1[j], lambda_k1[j], lambda_q2[j], lambda_k2[j]]).astype(F32)
            mix_c = _diff_attention(cq, ckT, cv, lam_vecs, c_sub_norm[j].reshape(1, LANES), bsz, seq,
                                    tq=256, tk=1024, lam_init=lam_init)
            mixes, wo = (mix_c,), w_out_odd[j]
        x2 = _ffn_block(x2, mixes, wo.astype(BF16), ffn_norm[i].reshape(1, d), w_gate[i].astype(BF16),
                        w_up[i].astype(BF16), w_down[i].astype(BF16), final_norm.reshape(1, d),
                        final=(i == depth - 1), tm=tm)
    return x2.reshape(bsz, seq, d)
```
